```python
import math
import jax, jax.numpy as jnp
from jax import lax
import numpy as np

D_MODEL = 1024
BATCH = 8
SEQ = 4096
DEPTH = 2

CHUNK = 64
Q_BLOCK = 128
EPS = 1e-6

A_HEADS = 4
A_DK = 64
A_DV = 2 * A_DK
B_HEADS = 4
B_DH = 64
IDX_HEADS = 8
IDX_DIM = 32
DSA_TOPK_MAX = 256
C_HEADS = 4
C_Q_LORA = 256
C_KV_LORA = 128
C_NOPE = 32
C_ROPE = 16
C_DV = 64
ROPE_THETA = 10000.0
REL_BUCKETS = 32
REL_MAX_DIST = 128
N_GROUPS = 4
EXPERTS_PER_GROUP = 8
N_EXPERTS = N_GROUPS * EXPERTS_PER_GROUP
TOP_K_IN_GROUP = 2
D_EXPERT = 512
MOE_BLOCK = 256

IN_SPLITS = (A_HEADS * 2 * A_DK, A_HEADS * 2 * A_DK, A_HEADS * A_DV,
             B_HEADS * B_DH, B_HEADS * B_DH, B_HEADS * B_DH,
             IDX_HEADS * IDX_DIM, IDX_DIM, IDX_HEADS,
             C_Q_LORA, C_KV_LORA, C_ROPE)
D_IN = sum(IN_SPLITS)
D_MIX = A_HEADS * A_DV + B_HEADS * B_DH + C_HEADS * C_DV

kernel_name = "hymba_style_diff_dsa_mla_hiermoe"


def _rms_norm(x, g):
    xf = x.astype(jnp.float32)
    y = xf * lax.rsqrt(jnp.mean(xf * xf, axis=-1, keepdims=True) + EPS)
    return (y * g.astype(jnp.float32)).astype(x.dtype)


def _t5_bucket(rel):
    half = REL_BUCKETS // 2
    exact = half // 2
    ret = jnp.where(rel > 0, half, 0)
    n = jnp.abs(rel)
    nf = jnp.maximum(n, 1).astype(jnp.float32)
    large = exact + (jnp.log(nf / exact) / math.log(REL_MAX_DIST / exact) * (half - exact)).astype(jnp.int32)
    large = jnp.minimum(large, half - 1)
    return ret + jnp.where(n < exact, n, large)


def _chunk_mask(q0, n_keys):
    t = q0 + jnp.arange(Q_BLOCK)
    s = jnp.arange(n_keys)
    return (s[None, :] // CHUNK) <= (t[:, None] // CHUNK)


def _rel_bias(table, q0, n_keys):
    t = q0 + jnp.arange(Q_BLOCK)
    s = jnp.arange(n_keys)
    b = table[_t5_bucket(s[None, :] - t[:, None])]
    return jnp.transpose(b, (2, 0, 1)).astype(jnp.float32)


def _rope_tables(seq):
    pos = jnp.arange(seq, dtype=jnp.float32)
    inv = ROPE_THETA ** (-jnp.arange(0, C_ROPE, 2, dtype=jnp.float32) / C_ROPE)
    ang = pos[:, None] * inv[None, :]
    return jnp.cos(ang), jnp.sin(ang)


def _rope(x, cos, sin):
    xf = x.astype(jnp.float32)
    x1, x2 = jnp.split(xf, 2, axis=-1)
    return jnp.concatenate([x1 * cos - x2 * sin, x1 * sin + x2 * cos], axis=-1).astype(x.dtype)


def _diff_attention(q, k, v, lam, lam_init, sub_gain, table):
    seq = q.shape[1]
    outs = []
    for q0 in range(0, seq, Q_BLOCK):
        n_keys = q0 + Q_BLOCK
        logits = jnp.einsum('bqhmd,bkhmd->bhmqk', q[:, q0:n_keys], k[:, :n_keys],
                            preferred_element_type=jnp.float32) * (A_DK ** -0.5)
        logits = logits + _rel_bias(table, q0, n_keys)[None, :, None]
        logits = jnp.where(_chunk_mask(q0, n_keys), logits, -jnp.inf)
        p = jax.nn.softmax(logits, axis=-1)
        a = p[:, :, 0] - lam * p[:, :, 1]
        outs.append(jnp.einsum('bhqk,bkhd->bqhd', a.astype(v.dtype), v[:, :n_keys]))
    o = jnp.concatenate(outs, axis=1)
    return _rms_norm(o, sub_gain) * (1.0 - lam_init)


def _dsa_attention(q, k, v, q_idx, k_idx, w_idx, table):
    seq = q.shape[1]
    topk = min(DSA_TOPK_MAX, seq // 4)
    outs = []
    for q0 in range(0, seq, Q_BLOCK):
        q1 = q0 + Q_BLOCK
        n_idx = max(q1, topk)
        t = q0 + jnp.arange(Q_BLOCK)
        s = jnp.arange(n_idx)
        adm = (s[None, :] // CHUNK) <= (t[:, None] // CHUNK)
        dots = jnp.einsum('bqhd,bkd->bqhk', q_idx[:, q0:q1], k_idx[:, :n_idx],
                          preferred_element_type=jnp.float32) * (IDX_DIM ** -0.5)
        w = w_idx[:, q0:q1].astype(jnp.float32) * (IDX_HEADS ** -0.5)
        score = jnp.einsum('bqh,bqhk->bqk', w, jax.nn.relu(dots))
        score = jnp.where(adm[None], score, -jnp.inf)
        vals, sel = lax.top_k(score, topk)
        valid = jnp.isfinite(vals)
        k_sel = jax.vmap(lambda kk, ii: kk[ii])(k, sel)
        v_sel = jax.vmap(lambda vv, ii: vv[ii])(v, sel)
        bias = table[_t5_bucket(sel - t[None, :, None])].astype(jnp.float32)
        logits = jnp.einsum('bqhd,bqkhd->bqhk', q[:, q0:q1], k_sel,
                            preferred_element_type=jnp.float32) * (B_DH ** -0.5)
        logits = logits + jnp.swapaxes(bias, 2, 3)
        logits = jnp.where(valid[:, :, None, :], logits, -jnp.inf)
        p = jax.nn.softmax(logits, axis=-1)
        outs.append(jnp.einsum('bqhk,bqkhd->bqhd', p.astype(v.dtype), v_sel))
    return jnp.concatenate(outs, axis=1)


def _mla(c_q, c_kv, k_r, q_gain, kv_gain, w_uq, w_ukv, cos, sin):
    bsz, seq, _ = c_q.shape
    q = (_rms_norm(c_q, q_gain) @ w_uq).reshape(bsz, seq, C_HEADS, C_NOPE + C_ROPE)
    q_nope = q[..., :C_NOPE]
    q_rope = _rope(q[..., C_NOPE:], cos[:, None], sin[:, None])
    kv = (_rms_norm(c_kv, kv_gain) @ w_ukv).reshape(bsz, seq, C_HEADS, C_NOPE + C_DV)
    k_nope, v = kv[..., :C_NOPE], kv[..., C_NOPE:]
    k_rope = _rope(k_r, cos, sin)
    scale = (C_NOPE + C_ROPE) ** -0.5
    outs = []
    for q0 in range(0, seq, Q_BLOCK):
        n_keys = q0 + Q_BLOCK
        logits = (jnp.einsum('bqhd,bkhd->bhqk', q_nope[:, q0:n_keys], k_nope[:, :n_keys],
                             preferred_element_type=jnp.float32)
                  + jnp.einsum('bqhr,bkr->bhqk', q_rope[:, q0:n_keys], k_rope[:, :n_keys],
                               preferred_element_type=jnp.float32)) * scale
        logits = jnp.where(_chunk_mask(q0, n_keys), logits, -jnp.inf)
        p = jax.nn.softmax(logits, axis=-1)
        outs.append(jnp.einsum('bhqk,bkhd->bqhd', p.astype(v.dtype), v[:, :n_keys]))
    return jnp.concatenate(outs, axis=1)


def _hier_moe(x, w_group, b_group, w_expert, b_expert, w_gate, w_up, w_down):
    bsz, seq, d = x.shape
    n = bsz * seq
    xf = x.reshape(n, d)
    xr = xf.astype(jnp.float32)
    g_prob = jax.nn.softmax(xr @ w_group.astype(jnp.float32) + b_group.astype(jnp.float32), axis=-1)
    g_p, g_sel = lax.top_k(g_prob, 1)
    e_logits = (xr @ w_expert.astype(jnp.float32) + b_expert.astype(jnp.float32))
    e_logits = e_logits.reshape(n, N_GROUPS, EXPERTS_PER_GROUP)
    e_logits = jnp.take_along_axis(e_logits, g_sel[:, :, None], axis=1)[:, 0]
    e_p, e_sel = lax.top_k(jax.nn.softmax(e_logits, axis=-1), TOP_K_IN_GROUP)
    gate = g_p * e_p / jnp.sum(e_p, axis=-1, keepdims=True)
    expert = g_sel * EXPERTS_PER_GROUP + e_sel
    n_assign = n * TOP_K_IN_GROUP
    flat_e = expert.reshape(-1)
    flat_tok = jnp.repeat(jnp.arange(n), TOP_K_IN_GROUP)
    flat_gate = gate.reshape(-1)
    order = jnp.argsort(flat_e)
    se, stok, sgate = flat_e[order], flat_tok[order], flat_gate[order]
    counts = jnp.bincount(flat_e, length=N_EXPERTS)
    padded = (counts + MOE_BLOCK - 1) // MOE_BLOCK * MOE_BLOCK
    pad_end = jnp.cumsum(padded)
    pad_start = pad_end - padded
    raw_start = jnp.cumsum(counts) - counts
    dest = pad_start[se] + (jnp.arange(n_assign) - raw_start[se])
    n_rows = (n_assign + MOE_BLOCK - 1) // MOE_BLOCK * MOE_BLOCK + N_EXPERTS * MOE_BLOCK
    n_blocks = n_rows // MOE_BLOCK
    buf = jnp.zeros((n_rows, d), x.dtype).at[dest].set(xf[stok])
    blk_e = jnp.minimum(jnp.searchsorted(pad_end, jnp.arange(n_blocks) * MOE_BLOCK, side='right'),
                        N_EXPERTS - 1)

    def expert_block(args):
        xb, e = args
        h = jax.nn.silu(xb @ w_gate[e]) * (xb @ w_up[e])
        return h @ w_down[e]

    y = lax.map(expert_block, (buf.reshape(n_blocks, MOE_BLOCK, d), blk_e)).reshape(n_rows, d)
    y = y[dest] * sgate[:, None].astype(y.dtype)
    out = jax.ops.segment_sum(y, stok, num_segments=n)
    return out.reshape(bsz, seq, d)


def setup_inputs(seed: int = 0) -> dict:
    key = jax.random.key(seed)
    ks = jax.random.split(key, 24)
    f32 = jnp.float32

    def nrm(k, shape, scale):
        return jax.random.normal(k, shape, f32) * scale

    def gain(k, shape):
        return 1.0 + 0.02 * jax.random.normal(k, shape, f32)

    return {
        "x": nrm(ks[0], (BATCH, SEQ, D_MODEL), 1.0),
        "w_in": nrm(ks[1], (DEPTH, D_MODEL, D_IN), D_MODEL ** -0.5),
        "w_out": nrm(ks[2], (DEPTH, D_MIX, D_MODEL), D_MIX ** -0.5),
        "norm_mix": gain(ks[3], (DEPTH, D_MODEL)),
        "norm_ffn": gain(ks[4], (DEPTH, D_MODEL)),
        "rel_bias": nrm(ks[5], (REL_BUCKETS, A_HEADS + B_HEADS), 0.5),
        "lam_q1": nrm(ks[6], (DEPTH, A_DK), 0.1),
        "lam_k1": nrm(ks[7], (DEPTH, A_DK), 0.1),
        "lam_q2": nrm(ks[8], (DEPTH, A_DK), 0.1),
        "lam_k2": nrm(ks[9], (DEPTH, A_DK), 0.1),
        "diff_subln": gain(ks[10], (DEPTH, A_DV)),
        "mla_q_norm": gain(ks[11], (DEPTH, C_Q_LORA)),
        "mla_kv_norm": gain(ks[12], (DEPTH, C_KV_LORA)),
        "w_uq": nrm(ks[13], (DEPTH, C_Q_LORA, C_HEADS * (C_NOPE + C_ROPE)), C_Q_LORA ** -0.5),
        "w_ukv": nrm(ks[14], (DEPTH, C_KV_LORA, C_HEADS * (C_NOPE + C_DV)), C_KV_LORA ** -0.5),
        "w_group": nrm(ks[15], (DEPTH, D_MODEL, N_GROUPS), D_MODEL ** -0.5),
        "b_group": nrm(ks[16], (DEPTH, N_GROUPS), 0.01),
        "w_expert": nrm(ks[17], (DEPTH, D_MODEL, N_EXPERTS), D_MODEL ** -0.5),
        "b_expert": nrm(ks[18], (DEPTH, N_EXPERTS), 0.01),
        "w_gate": nrm(ks[19], (DEPTH, N_EXPERTS, D_MODEL, D_EXPERT), D_MODEL ** -0.5),
        "w_up": nrm(ks[20], (DEPTH, N_EXPERTS, D_MODEL, D_EXPERT), D_MODEL ** -0.5),
        "w_down": nrm(ks[21], (DEPTH, N_EXPERTS, D_EXPERT, D_MODEL), D_EXPERT ** -0.5),
        "norm_final": gain(ks[22], (D_MODEL,)),
    }


def reference(x, w_in, w_out, norm_mix, norm_ffn, rel_bias, lam_q1, lam_k1, lam_q2, lam_k2,
              diff_subln, mla_q_norm, mla_kv_norm, w_uq, w_ukv, w_group, b_group, w_expert,
              b_expert, w_gate, w_up, w_down, norm_final):
    bsz, seq, _ = x.shape
    cos, sin = _rope_tables(seq)
    split_at = np.cumsum(IN_SPLITS)[:-1].tolist()
    table_a = rel_bias[:, :A_HEADS]
    table_b = rel_bias[:, A_HEADS:]
    for l in range(DEPTH):
        h = _rms_norm(x, norm_mix[l])
        z = h @ w_in[l]
        (aq, ak, av, bq, bk, bv, iq, ik, iw, cq, ckv, ckr) = jnp.split(z, split_at, axis=-1)
        lam_init = 0.8 - 0.6 * math.exp(-0.3 * l)
        lam = (jnp.exp(jnp.sum(lam_q1[l].astype(jnp.float32) * lam_k1[l].astype(jnp.float32)))
               - jnp.exp(jnp.sum(lam_q2[l].astype(jnp.float32) * lam_k2[l].astype(jnp.float32)))
               + lam_init)
        ya = _diff_attention(aq.reshape(bsz, seq, A_HEADS, 2, A_DK),
                             ak.reshape(bsz, seq, A_HEADS, 2, A_DK),
                             av.reshape(bsz, seq, A_HEADS, A_DV),
                             lam, lam_init, diff_subln[l], table_a)
        yb = _dsa_attention(bq.reshape(bsz, seq, B_HEADS, B_DH),
                            bk.reshape(bsz, seq, B_HEADS, B_DH),
                            bv.reshape(bsz, seq, B_HEADS, B_DH),
                            iq.reshape(bsz, seq, IDX_HEADS, IDX_DIM), ik, iw, table_b)
        yc = _mla(cq, ckv, ckr, mla_q_norm[l], mla_kv_norm[l], w_uq[l], w_ukv[l], cos, sin)
        mix = jnp.concatenate([ya.reshape(bsz, seq, -1), yb.reshape(bsz, seq, -1),
                               yc.reshape(bsz, seq, -1)], axis=-1)
        x = x + mix @ w_out[l]
        x = x + _hier_moe(_rms_norm(x, norm_ffn[l]), w_group[l], b_group[l], w_expert[l],
                          b_expert[l], w_gate[l], w_up[l], w_down[l])
    return _rms_norm(x, norm_final)
```

```python
import functools
import math

import numpy as np
import jax
import jax.numpy as jnp
from jax import lax
from jax.experimental import pallas as pl
from jax.experimental.pallas import tpu as pltpu

F32 = jnp.float32
BF16 = jnp.bfloat16

D_MODEL = 1024
DEPTH = 2
CHUNK = 64
Q_BLOCK = 128
EPS = 1e-6
A_HEADS, A_DK, A_DV = 4, 64, 128
B_HEADS, B_DH = 4, 64
IDX_HEADS, IDX_DIM = 8, 32
DSA_TOPK_MAX = 256
C_HEADS, C_Q_LORA, C_KV_LORA, C_NOPE, C_ROPE, C_DV = 4, 256, 128, 32, 16, 64
ROPE_THETA = 10000.0
REL_BUCKETS, REL_MAX_DIST = 32, 128
N_GROUPS, EXPERTS_PER_GROUP = 4, 8
N_EXPERTS = N_GROUPS * EXPERTS_PER_GROUP
D_EXPERT = 512
IN_SPLITS = (A_HEADS * 2 * A_DK, A_HEADS * 2 * A_DK, A_HEADS * A_DV,
             B_HEADS * B_DH, B_HEADS * B_DH, B_HEADS * B_DH,
             IDX_HEADS * IDX_DIM, IDX_DIM, IDX_HEADS,
             C_Q_LORA, C_KV_LORA, C_ROPE)

LANES = 128
MASKED = -1e30
VMEM_LIMIT = 48 * 1024 * 1024

ZB_AQ, ZB_AK, ZB_AV = 0, 4, 8
ZB_BQ, ZB_BK, ZB_BV = 12, 14, 16
ZB_IQ, ZB_CQ, ZB_CKV, ZB_IK, ZB_MISC = 18, 20, 22, 23, 24
Z_BLOCKS = 25
Z_WIDTH = Z_BLOCKS * LANES
MISC_IW, MISC_CKR = 0, IDX_HEADS

MOE_ROWS = 256
TM = 512
T_ROWS = 256


def _cparams(n_grid):
    return pltpu.CompilerParams(dimension_semantics=("arbitrary",) * n_grid,
                                vmem_limit_bytes=VMEM_LIMIT)


def _lane_iota(shape):
    return lax.broadcasted_iota(jnp.int32, shape, 1)


def _in_proj_kernel(x_ref, g_ref, w_ref, z_ref, *, n_chunk):
    x = x_ref[...]
    h = x * lax.rsqrt(jnp.mean(x * x, axis=-1, keepdims=True) + EPS) * g_ref[...]
    h = h.astype(BF16)
    for c in range(0, Z_WIDTH, n_chunk):
        z_ref[:, c:c + n_chunk] = jnp.dot(h, w_ref[:, c:c + n_chunk],
                                          preferred_element_type=F32).astype(BF16)


def _in_proj(x2d, gain, w_p):
    n = x2d.shape[0]
    return pl.pallas_call(
        functools.partial(_in_proj_kernel, n_chunk=5 * LANES),
        grid=(n // TM,),
        in_specs=[pl.BlockSpec((TM, D_MODEL), lambda i: (i, 0)),
                  pl.BlockSpec((1, D_MODEL), lambda i: (0, 0)),
                  pl.BlockSpec((D_MODEL, Z_WIDTH), lambda i: (0, 0))],
        out_specs=pl.BlockSpec((TM, Z_WIDTH), lambda i: (i, 0)),
        out_shape=jax.ShapeDtypeStruct((n, Z_WIDTH), BF16),
        compiler_params=_cparams(1),
        name="in_proj",
    )(x2d, gain.reshape(1, D_MODEL), w_p)


def _mla_prep_kernel(cq_ref, ckv_ref, misc_ref, cos_ref, sin_ref, gq_ref, gkv_ref,
                     wq_ref, wqs_ref, wk_ref, wv_ref, e1_ref, e2_ref,
                     q_ref, k_ref, v_ref):
    def rms(v, g):
        return (v * lax.rsqrt(jnp.mean(v * v, axis=-1, keepdims=True) + EPS) * g).astype(BF16)

    def mm(a, w_ref_):
        return jnp.dot(a, w_ref_[...], preferred_element_type=F32)

    cos, sin = cos_ref[...], sin_ref[...]
    qn = rms(cq_ref[...].astype(F32), gq_ref[...])
    q_ref[...] = (mm(qn, wq_ref) * cos + mm(qn, wqs_ref) * sin).astype(BF16)
    kvn = rms(ckv_ref[...].astype(F32), gkv_ref[...])
    misc = misc_ref[...]
    k_ref[...] = (mm(kvn, wk_ref) + mm(misc, e1_ref) * cos + mm(misc, e2_ref) * sin).astype(BF16)
    v_ref[...] = mm(kvn, wv_ref).astype(BF16)


def _mla_prep(z2d, seq, cos_t, sin_t, gq, gkv, wq, wqs, wk, wv, e1, e2):
    n = z2d.shape[0]
    tiles_per_seq = seq // TM
    full = lambda a: pl.BlockSpec(a.shape, lambda i: (0, 0))
    out = jax.ShapeDtypeStruct((n, 2 * LANES), BF16)
    return pl.pallas_call(
        _mla_prep_kernel,
        grid=(n // TM,),
        in_specs=[pl.BlockSpec((TM, 2 * LANES), lambda i: (i, ZB_CQ // 2)),
                  pl.BlockSpec((TM, LANES), lambda i: (i, ZB_CKV)),
                  pl.BlockSpec((TM, LANES), lambda i: (i, ZB_MISC)),
                  pl.BlockSpec((TM, 2 * LANES), lambda i: (i % tiles_per_seq, 0)),
                  pl.BlockSpec((TM, 2 * LANES), lambda i: (i % tiles_per_seq, 0)),
                  full(gq), full(gkv), full(wq), full(wqs), full(wk), full(wv), full(e1), full(e2)],
        out_specs=[pl.BlockSpec((TM, 2 * LANES), lambda i: (i, 0))] * 3,
        out_shape=[out, out, out],
        compiler_params=_cparams(1),
        name="mla_prep",
    )(z2d, z2d, z2d, cos_t, sin_t, gq, gkv, wq, wqs, wk, wv, e1, e2)


def _half_mask(q_f32, lo, width):
    lane = _lane_iota(q_f32.shape)
    return jnp.where((lane >= lo) & (lane < lo + width), q_f32, 0.0).astype(BF16)


def _nt_dot(a, b):
    return lax.dot_general(a, b, (((1,), (1,)), ((), ())), preferred_element_type=F32)


def _flash_head(qm, k_ref, v_ref, kblk, vblk, bias_ref, farb, bidx, qb, scale, mask_ref):
    kc = slice(kblk * LANES, (kblk + 1) * LANES)
    vc = slice(vblk * LANES, (vblk + 1) * LANES)

    def step(kt, bias, carry):
        m, l, acc = carry
        k0 = pl.multiple_of(kt * Q_BLOCK, Q_BLOCK)
        s = _nt_dot(qm, k_ref[0, pl.ds(k0, Q_BLOCK), kc]) * scale + bias
        if mask_ref is not None:
            s = s + mask_ref[:, pl.ds(k0, Q_BLOCK)]
        m_new = jnp.maximum(m, jnp.max(s, axis=1, keepdims=True))
        alpha = jnp.exp(m - m_new)
        p = jnp.exp(s - m_new)
        l = alpha * l + jnp.sum(p, axis=1, keepdims=True)
        acc = alpha * acc + jnp.dot(p.astype(BF16), v_ref[0, pl.ds(k0, Q_BLOCK), vc],
                                    preferred_element_type=F32)
        return m_new, l, acc

    init = (jnp.full((Q_BLOCK, 1), MASKED, F32), jnp.zeros((Q_BLOCK, 1), F32),
            jnp.zeros((Q_BLOCK, LANES), F32))
    n_far = jnp.maximum(qb - 1, 0)
    carry = lax.fori_loop(0, n_far, lambda kt, c: step(kt, farb, c), init)
    carry = lax.fori_loop(n_far, qb + 1,
                          lambda kt, c: step(kt, bias_ref[bidx, qb - kt], c), carry)
    _, l, acc = carry
    return acc, l


def _diff_attn_kernel(sc_ref, q_ref, k_ref, v_ref, subg_ref, bias_ref, o_ref):
    qb = pl.program_id(1)
    lam, out_scale = sc_ref[0], sc_ref[1]
    for h in range(A_HEADS):
        q = q_ref[0, :, h * LANES:(h + 1) * LANES].astype(F32)
        outs = []
        for m in range(2):
            qm = _half_mask(q, m * A_DK, A_DK)
            acc, l = _flash_head(qm, k_ref, v_ref, h, h, bias_ref, sc_ref[2 + h], h, qb,
                                 A_DK ** -0.5, None)
            outs.append(acc / l)
        o = outs[0] - lam * outs[1]
        o = o * lax.rsqrt(jnp.mean(o * o, axis=-1, keepdims=True) + EPS) * subg_ref[...]
        o_ref[0, :, h * LANES:(h + 1) * LANES] = (o * out_scale).astype(o_ref.dtype)


def _pair_heads_attn(q_ref, k_ref, v_ref, bias_ref, farb_of, bidx_of, o_ref, qb, n_heads, dh,
                     scale, mask_ref):
    for pair in range(n_heads // 2):
        q = q_ref[0, :, pair * LANES:(pair + 1) * LANES].astype(F32)
        res = []
        for j in range(2):
            h = 2 * pair + j
            qm = _half_mask(q, j * dh, dh)
            acc, l = _flash_head(qm, k_ref, v_ref, pair, pair, bias_ref, farb_of(h), bidx_of(h),
                                 qb, scale, mask_ref)
            res.append(acc / l)
        lane = _lane_iota(res[0].shape)
        o_ref[0, :, pair * LANES:(pair + 1) * LANES] = jnp.where(
            lane < dh, res[0], res[1]).astype(o_ref.dtype)


def _mla_attn_kernel(sc_ref, q_ref, k_ref, v_ref, bias_ref, o_ref):
    qb = pl.program_id(1)
    _pair_heads_attn(q_ref, k_ref, v_ref, bias_ref, lambda h: sc_ref[0], lambda h: 0, o_ref, qb,
                     C_HEADS, C_DV, (C_NOPE + C_ROPE) ** -0.5, None)


def _float_of_ordered(u):
    key = u ^ jnp.int32(-2 ** 31)
    bits = key ^ ((key >> 31) & jnp.int32(0x7FFFFFFF))
    return lax.bitcast_convert_type(bits, F32)


def _dsa_attn_kernel(sc_ref, q_ref, k_ref, v_ref, iq_ref, iwq_ref, ik_ref, bias_ref, o_ref,
                     s_ref, *, topk):
    qb = pl.program_id(1)
    n_tiles = qb + 1

    def tile(kt):
        return pl.ds(pl.multiple_of(kt * Q_BLOCK, Q_BLOCK), Q_BLOCK)

    iq = [iq_ref[0, :, b * LANES:(b + 1) * LANES].astype(F32) for b in range(2)]
    per_blk = LANES // IDX_DIM
    iq_heads = [_half_mask(iq[h // per_blk], (h % per_blk) * IDX_DIM, IDX_DIM)
                for h in range(IDX_HEADS)]
    iw = iwq_ref[0].astype(F32) * (IDX_HEADS ** -0.5)
    iw_heads = [iw[:, MISC_IW + h:MISC_IW + h + 1] for h in range(IDX_HEADS)]
    row = lax.broadcasted_iota(jnp.int32, (Q_BLOCK, Q_BLOCK), 0)
    col = _lane_iota((Q_BLOCK, Q_BLOCK))
    admissible = (col // CHUNK) <= (row // CHUNK)

    def score_tile(kt, _):
        ik = ik_ref[0, tile(kt), :]
        sc = jnp.zeros((Q_BLOCK, Q_BLOCK), F32)
        for h in range(IDX_HEADS):
            dots = _nt_dot(iq_heads[h], ik) * (IDX_DIM ** -0.5)
            sc = sc + iw_heads[h] * jnp.maximum(dots, 0.0)
        penalty = jnp.where(kt < qb, 0.0, -jnp.inf)
        s_ref[:, tile(kt)] = sc + jnp.where(admissible, 0.0, penalty)
        return 0

    lax.fori_loop(0, n_tiles, score_tile, 0)

    @pl.when(n_tiles * Q_BLOCK <= topk)
    def _():
        def body(kt, _):
            s = s_ref[:, tile(kt)]
            s_ref[:, tile(kt)] = jnp.where(s == -jnp.inf, MASKED, 0.0)
            return 0
        lax.fori_loop(0, n_tiles, body, 0)

    @pl.when(n_tiles * Q_BLOCK > topk)
    def _():
        kf = float(topk)
        col_f = col.astype(F32)

        def key_index(kt):
            return (kt * Q_BLOCK).astype(F32) + col_f

        def count_ge(p):
            def body(kt, acc):
                return acc + jnp.where(s_ref[:, tile(kt)] >= p, 1.0, 0.0)
            acc = lax.fori_loop(0, n_tiles, body, jnp.zeros((Q_BLOCK, Q_BLOCK), F32))
            return jnp.sum(acc, axis=1, keepdims=True)

        def bit_step(i, u):
            cand = u | lax.shift_left(jnp.int32(1), 31 - i)
            return jnp.where(count_ge(_float_of_ordered(cand)) >= kf, cand, u)

        u = lax.fori_loop(0, 32, bit_step, jnp.zeros((Q_BLOCK, 1), jnp.int32))
        p_lo = _float_of_ordered(u)
        p_hi = _float_of_ordered(u + 1)
        need0 = kf - count_ge(p_hi)

        def remaining(s, idx, vstar, jstar):
            bucket = (s >= p_lo) & jnp.logical_not(s >= p_hi)
            after = (s < vstar) | ((s == vstar) & (idx > jstar))
            return bucket & after

        def pick(state):
            vstar, jstar, need, _ = state

            def vmax_body(kt, acc):
                s = s_ref[:, tile(kt)]
                rem = remaining(s, key_index(kt), vstar, jstar)
                return jnp.maximum(acc, jnp.where(rem, s, -jnp.inf))
            bv = jnp.max(lax.fori_loop(0, n_tiles, vmax_body,
                                       jnp.full((Q_BLOCK, Q_BLOCK), -jnp.inf, F32)),
                         axis=1, keepdims=True)

            def imin_body(kt, acc):
                s = s_ref[:, tile(kt)]
                idx = key_index(kt)
                hit = remaining(s, idx, vstar, jstar) & (s == bv)
                return jnp.minimum(acc, jnp.where(hit, idx, jnp.inf))
            bi = jnp.min(lax.fori_loop(0, n_tiles, imin_body,
                                       jnp.full((Q_BLOCK, Q_BLOCK), jnp.inf, F32)),
                         axis=1, keepdims=True)
            active = need > 0.0
            vstar = jnp.where(active, bv, vstar)
            jstar = jnp.where(active, bi, jstar)
            need = jnp.where(active, need - 1.0, need)
            return vstar, jstar, need, jnp.max(need)

        state0 = (jnp.full((Q_BLOCK, 1), jnp.inf, F32), jnp.full((Q_BLOCK, 1), -1.0, F32),
                  need0, jnp.max(need0))
        vstar, jstar, _, _ = lax.while_loop(lambda st: st[3] > 0.0, pick, state0)

        def mask_body(kt, _):
            s = s_ref[:, tile(kt)]
            sel = (s >= p_lo) & jnp.logical_not(remaining(s, key_index(kt), vstar, jstar))
            s_ref[:, tile(kt)] = jnp.where(sel, 0.0, MASKED)
            return 0
        lax.fori_loop(0, n_tiles, mask_body, 0)

    _pair_heads_attn(q_ref, k_ref, v_ref, bias_ref, lambda h: sc_ref[h], lambda h: h, o_ref, qb,
                     B_HEADS, B_DH, B_DH ** -0.5, s_ref)


def _attn_call(kernel_fn, name, scalars, bsz, seq, q_arr, q_blk, q_w, kv_specs, extra, bias,
               out_w, scratch=()):
    spec = lambda a, bs, im: pl.BlockSpec(bs, im)
    in_arrays = [scalars, q_arr] + [a for a, _, _ in kv_specs + extra] + [bias]
    in_specs = ([pl.BlockSpec(memory_space=pltpu.SMEM),
                 pl.BlockSpec((1, Q_BLOCK, q_w), lambda b, i: (b, i, q_blk))]
                + [spec(a, bs, im) for a, bs, im in kv_specs + extra]
                + [pl.BlockSpec(bias.shape, lambda b, i: (0,) * bias.ndim)])
    return pl.pallas_call(
        kernel_fn,
        grid=(bsz, seq // Q_BLOCK),
        in_specs=in_specs,
        out_specs=pl.BlockSpec((1, Q_BLOCK, out_w), lambda b, i: (b, i, 0)),
        out_shape=jax.ShapeDtypeStruct((bsz, seq, out_w), BF16),
        scratch_shapes=list(scratch),
        compiler_params=_cparams(2),
        name=name,
    )(*in_arrays)


def _out_router_kernel(ya_ref, yb_ref, yc_ref, x_ref, wo_ref, g_ref, wrh_ref, wrl_ref, br_ref,
                       x1_ref, hn_ref, route_ref):
    na, nb = A_HEADS * A_DV, B_HEADS * B_DH
    mix = (jnp.dot(ya_ref[...], wo_ref[0:na, :], preferred_element_type=F32)
           + jnp.dot(yb_ref[...], wo_ref[na:na + nb, :], preferred_element_type=F32)
           + jnp.dot(yc_ref[...], wo_ref[na + nb:, :], preferred_element_type=F32))
    x1 = x_ref[...] + mix
    x1_ref[...] = x1
    hn = x1 * lax.rsqrt(jnp.mean(x1 * x1, axis=-1, keepdims=True) + EPS) * g_ref[...]
    hn_ref[...] = hn
    h_hi = hn.astype(BF16)
    h_lo = (hn - h_hi.astype(F32)).astype(BF16)
    logits = (jnp.dot(h_hi, wrh_ref[...], preferred_element_type=F32)
              + jnp.dot(h_lo, wrh_ref[...], preferred_element_type=F32)
              + jnp.dot(h_hi, wrl_ref[...], preferred_element_type=F32)) + br_ref[...]
    lane = _lane_iota(logits.shape)
    lane_f = lane.astype(F32)
    ninf = -jnp.inf
    gl = jnp.where(lane < N_GROUPS, logits, ninf)
    gmax = jnp.max(gl, axis=1, keepdims=True)
    g_p = 1.0 / jnp.sum(jnp.exp(gl - gmax), axis=1, keepdims=True)
    g_sel = jnp.min(jnp.where(gl == gmax, lane_f, float(LANES)), axis=1, keepdims=True)
    lo = float(N_GROUPS) + float(EXPERTS_PER_GROUP) * g_sel
    in_group = (lane_f >= lo) & (lane_f < lo + float(EXPERTS_PER_GROUP))
    el = jnp.where(in_group, logits, ninf)
    ee = jnp.exp(el - jnp.max(el, axis=1, keepdims=True))
    prob = jnp.where(in_group, ee / jnp.sum(ee, axis=1, keepdims=True), -1.0)
    p0 = jnp.max(prob, axis=1, keepdims=True)
    i0 = jnp.min(jnp.where(prob == p0, lane_f, float(LANES)), axis=1, keepdims=True)
    prob1 = jnp.where(lane_f == i0, -1.0, prob)
    p1 = jnp.max(prob1, axis=1, keepdims=True)
    i1 = jnp.min(jnp.where(prob1 == p1, lane_f, float(LANES)), axis=1, keepdims=True)
    den = p0 + p1
    route = jnp.where(lane == 0, g_p * p0 / den,
                      jnp.where(lane == 1, g_p * p1 / den,
                                jnp.where(lane == 2, i0 - float(N_GROUPS),
                                          jnp.where(lane == 3, i1 - float(N_GROUPS), 0.0))))
    route_ref[...] = route


def _out_router(ya, yb, yc, x2d, wo, g_ffn, wr_hi, wr_lo, br):
    n = x2d.shape[0]
    row = lambda w: pl.BlockSpec((TM, w), lambda i: (i, 0))
    full = lambda a: pl.BlockSpec(a.shape, lambda i: (0, 0))
    return pl.pallas_call(
        _out_router_kernel,
        grid=(n // TM,),
        in_specs=[row(ya.shape[1]), row(yb.shape[1]), row(yc.shape[1]), row(D_MODEL),
                  full(wo), full(g_ffn), full(wr_hi), full(wr_lo), full(br)],
        out_specs=[row(D_MODEL), row(D_MODEL), row(LANES)],
        out_shape=[jax.ShapeDtypeStruct((n, D_MODEL), F32),
                   jax.ShapeDtypeStruct((n, D_MODEL), F32),
                   jax.ShapeDtypeStruct((n, LANES), F32)],
        compiler_params=_cparams(1),
        name="out_router",
    )(ya, yb, yc, x2d, wo, g_ffn, wr_hi, wr_lo, br)


def _moe_pos_kernel(route_ref, tri_ref, dest_ref, counts_ref, base_ref, start_ref):
    phase, i = pl.program_id(0), pl.program_id(1)
    route = route_ref[...]
    lane = _lane_iota(route.shape)
    lane_f = lane.astype(F32)
    oh0 = jnp.where(lane_f == route[:, 2:3], 1.0, 0.0)
    oh1 = jnp.where(lane_f == route[:, 3:4], 1.0, 0.0)
    both = oh0 + oh1
    tile_counts = jnp.sum(both, axis=0, keepdims=True)

    @pl.when(jnp.logical_and(phase == 0, i == 0))
    def _():
        counts_ref[...] = jnp.zeros_like(counts_ref)

    @pl.when(phase == 0)
    def _():
        counts_ref[...] += tile_counts
        dest_ref[...] = jnp.zeros_like(dest_ref)

    @pl.when(jnp.logical_and(phase == 1, i == 0))
    def _():
        padded = jnp.ceil(counts_ref[...] * (1.0 / MOE_ROWS)) * MOE_ROWS
        incl = jnp.broadcast_to(padded, (8, LANES))
        lane8 = _lane_iota((8, LANES))
        shift = 1
        while shift < N_EXPERTS:
            incl = incl + jnp.where(lane8 >= shift, pltpu.roll(incl, shift, 1), 0.0)
            shift *= 2
        start_ref[...] = (incl - padded)[0:1]
        base_ref[...] = jnp.zeros_like(base_ref)

    @pl.when(phase == 1)
    def _():
        before = jnp.dot(tri_ref[...], both.astype(BF16), preferred_element_type=F32)
        slot = before + base_ref[...] + start_ref[...]
        d0 = jnp.sum(oh0 * slot, axis=1, keepdims=True)
        d1 = jnp.sum(oh1 * slot, axis=1, keepdims=True)
        dest_ref[...] = jnp.where(lane == 0, d0, jnp.where(lane == 1, d1, 0.0)).astype(jnp.int32)
        base_ref[...] += tile_counts


def _moe_positions(route):
    n = route.shape[0]
    tri = jnp.asarray(np.tril(np.ones((TM, TM), np.float32), -1), BF16)
    return pl.pallas_call(
        _moe_pos_kernel,
        grid=(2, n // TM),
        in_specs=[pl.BlockSpec((TM, LANES), lambda p, i: (i, 0)),
                  pl.BlockSpec((TM, TM), lambda p, i: (0, 0))],
        out_specs=[pl.BlockSpec((TM, LANES), lambda p, i: (i * p, 0)),
                   pl.BlockSpec((1, LANES), lambda p, i: (0, 0))],
        out_shape=[jax.ShapeDtypeStruct((n, LANES), jnp.int32),
                   jax.ShapeDtypeStruct((1, LANES), F32)],
        scratch_shapes=[pltpu.VMEM((1, LANES), F32), pltpu.VMEM((1, LANES), F32)],
        compiler_params=_cparams(2),
        name="moe_positions",
    )(route, tri)


def _dispatch_kernel(dest_ref, hn_hbm, buf_in, buf_hbm, sem):
    del buf_in
    t0 = pl.program_id(0) * T_ROWS

    def row_copy(t, d):
        return pltpu.make_async_copy(hn_hbm.at[pl.ds(t, 1)], buf_hbm.at[pl.ds(d, 1)], sem)

    def issue(r, _):
        t = t0 + r
        row_copy(t, dest_ref[2 * t]).start()
        row_copy(t, dest_ref[2 * t + 1]).start()
        return 0

    def drain(r, _):
        t = t0 + r
        row_copy(t, dest_ref[2 * t]).wait()
        row_copy(t, dest_ref[2 * t + 1]).wait()
        return 0

    lax.fori_loop(0, T_ROWS, issue, 0)
    lax.fori_loop(0, T_ROWS, drain, 0)


def _dispatch(dest_flat, hn, n_rows):
    n = hn.shape[0]
    buf0 = jnp.zeros((n_rows, D_MODEL), F32)
    return pl.pallas_call(
        _dispatch_kernel,
        grid_spec=pltpu.PrefetchScalarGridSpec(
            num_scalar_prefetch=1,
            grid=(n // T_ROWS,),
            in_specs=[pl.BlockSpec(memory_space=pl.ANY), pl.BlockSpec(memory_space=pl.ANY)],
            out_specs=pl.BlockSpec(memory_space=pl.ANY),
            scratch_shapes=[pltpu.SemaphoreType.DMA(())]),
        out_shape=jax.ShapeDtypeStruct((n_rows, D_MODEL), F32),
        input_output_aliases={2: 0},
        compiler_params=_cparams(1),
        name="moe_dispatch",
    )(dest_flat, hn, buf0)


def _expert_kernel(blk_e_ref, n_used_ref, x_ref, wg_ref, wu_ref, wd_ref, y_ref):
    del blk_e_ref

    @pl.when(pl.program_id(0) < n_used_ref[0])
    def _():
        xb = x_ref[...].astype(BF16)
        g = jnp.dot(xb, wg_ref[0], preferred_element_type=F32)
        u = jnp.dot(xb, wu_ref[0], preferred_element_type=F32)
        h = (g / (1.0 + jnp.exp(-g))) * u
        y_ref[...] = jnp.dot(h.astype(BF16), wd_ref[0], preferred_element_type=F32)

    @pl.when(pl.program_id(0) >= n_used_ref[0])
    def _():
        y_ref[...] = jnp.zeros_like(y_ref)


def _experts(blk_e, n_used, buf, wg, wu, wd):
    n_rows = buf.shape[0]
    return pl.pallas_call(
        _expert_kernel,
        grid_spec=pltpu.PrefetchScalarGridSpec(
            num_scalar_prefetch=2,
            grid=(n_rows // MOE_ROWS,),
            in_specs=[pl.BlockSpec((MOE_ROWS, D_MODEL), lambda i, be, nu: (i, 0)),
                      pl.BlockSpec((1, D_MODEL, D_EXPERT), lambda i, be, nu: (be[i], 0, 0)),
                      pl.BlockSpec((1, D_MODEL, D_EXPERT), lambda i, be, nu: (be[i], 0, 0)),
                      pl.BlockSpec((1, D_EXPERT, D_MODEL), lambda i, be, nu: (be[i], 0, 0))],
            out_specs=pl.BlockSpec((MOE_ROWS, D_MODEL), lambda i, be, nu: (i, 0))),
        out_shape=jax.ShapeDtypeStruct((n_rows, D_MODEL), F32),
        compiler_params=_cparams(1),
        name="moe_experts",
    )(blk_e, n_used, buf, wg, wu, wd)


def _combine_kernel(dest_ref, y_hbm, x1_ref, route_ref, gfin_ref, o_ref, ybuf, sem, *, final):
    t0 = pl.program_id(0) * T_ROWS

    def row_copy(r, j):
        d = dest_ref[2 * (t0 + r) + j]
        return pltpu.make_async_copy(y_hbm.at[pl.ds(d, 1)], ybuf.at[j, pl.ds(r, 1)], sem)

    def issue(r, _):
        row_copy(r, 0).start()
        row_copy(r, 1).start()
        return 0

    def drain(r, _):
        row_copy(r, 0).wait()
        row_copy(r, 1).wait()
        return 0

    lax.fori_loop(0, T_ROWS, issue, 0)
    lax.fori_loop(0, T_ROWS, drain, 0)
    route = route_ref[...]
    x2 = x1_ref[...] + (ybuf[0] * route[:, 0:1] + ybuf[1] * route[:, 1:2])
    if final:
        x2 = x2 * lax.rsqrt(jnp.mean(x2 * x2, axis=-1, keepdims=True) + EPS) * gfin_ref[...]
    o_ref[...] = x2


def _combine(dest_flat, y, x1, route, g_final, final):
    n = x1.shape[0]
    return pl.pallas_call(
        functools.partial(_combine_kernel, final=final),
        grid_spec=pltpu.PrefetchScalarGridSpec(
            num_scalar_prefetch=1,
            grid=(n // T_ROWS,),
            in_specs=[pl.BlockSpec(memory_space=pl.ANY),
                      pl.BlockSpec((T_ROWS, D_MODEL), lambda i, d: (i, 0)),
                      pl.BlockSpec((T_ROWS, LANES), lambda i, d: (i, 0)),
                      pl.BlockSpec((1, D_MODEL), lambda i, d: (0, 0))],
            out_specs=pl.BlockSpec((T_ROWS, D_MODEL), lambda i, d: (i, 0)),
            scratch_shapes=[pltpu.VMEM((2, T_ROWS, D_MODEL), F32), pltpu.SemaphoreType.DMA(())]),
        out_shape=jax.ShapeDtypeStruct((n, D_MODEL), F32),
        compiler_params=_cparams(1),
        name="moe_combine",
    )(dest_flat, y, x1, route, g_final)


def _t5_bucket(rel):
    half = REL_BUCKETS // 2
    exact = half // 2
    ret = jnp.where(rel > 0, half, 0)
    n = jnp.abs(rel)
    nf = jnp.maximum(n, 1).astype(jnp.float32)
    large = exact + (jnp.log(nf / exact) / math.log(REL_MAX_DIST / exact) * (half - exact)).astype(jnp.int32)
    large = jnp.minimum(large, half - 1)
    return ret + jnp.where(n < exact, n, large)


def _bias_tiles(table):
    i = jnp.arange(Q_BLOCK)[:, None]
    j = jnp.arange(Q_BLOCK)[None, :]
    diag = table[_t5_bucket(j - i)]
    diag = jnp.where(((j // CHUNK) <= (i // CHUNK))[..., None], diag, MASKED)
    prev = table[_t5_bucket(j - i - Q_BLOCK)]
    tiles = jnp.transpose(jnp.stack([diag, prev], axis=0), (3, 0, 1, 2)).astype(F32)
    far = table[_t5_bucket(jnp.int32(-2 * Q_BLOCK))].astype(F32)
    return tiles, far


def _rope_lane_tables(seq):
    pos = jnp.arange(seq, dtype=jnp.float32)
    inv = ROPE_THETA ** (-jnp.arange(0, C_ROPE, 2, dtype=jnp.float32) / C_ROPE)
    ang = pos[:, None] * inv[None, :]
    cos, sin = jnp.cos(ang), jnp.sin(ang)
    head_w = 2 * LANES // C_HEADS
    ones = jnp.ones((seq, C_NOPE), F32)
    zeros = jnp.zeros((seq, head_w - C_NOPE - C_ROPE), F32)
    cos_h = jnp.concatenate([ones, cos, cos, zeros], axis=1)
    sin_h = jnp.concatenate([0 * ones, sin, sin, zeros], axis=1)
    return jnp.tile(cos_h, (1, C_HEADS)), jnp.tile(sin_h, (1, C_HEADS))


def _prep_w_in(w):
    aq, ak, av, bq, bk, bv, iq, ik, iw, cq, ckv, ckr = jnp.split(
        w, np.cumsum(IN_SPLITS)[:-1].tolist(), axis=1)
    ik4 = jnp.concatenate([ik] * (LANES // IDX_DIM), axis=1)
    misc = jnp.concatenate([iw, ckr, jnp.zeros((w.shape[0], LANES - IDX_HEADS - C_ROPE), w.dtype)],
                           axis=1)
    return jnp.concatenate([aq, ak, av, bq, bk, bv, iq, cq, ckv, ik4, misc], axis=1).astype(BF16)


def _prep_mla_weights(w_uq, w_ukv):
    head_w = 2 * LANES // C_HEADS
    half = C_ROPE // 2
    dq = C_NOPE + C_ROPE
    wq = jnp.zeros((C_Q_LORA, 2 * LANES), F32)
    wqs = jnp.zeros((C_Q_LORA, 2 * LANES), F32)
    wk = jnp.zeros((C_KV_LORA, 2 * LANES), F32)
    wv = jnp.zeros((C_KV_LORA, 2 * LANES), F32)
    e1 = np.zeros((LANES, 2 * LANES), np.float32)
    e2 = np.zeros((LANES, 2 * LANES), np.float32)
    for h in range(C_HEADS):
        o = h * head_w
        wq = wq.at[:, o:o + dq].set(w_uq[:, h * dq:(h + 1) * dq])
        r = h * dq + C_NOPE
        wqs = wqs.at[:, o + C_NOPE:o + C_NOPE + half].set(-w_uq[:, r + half:r + C_ROPE])
        wqs = wqs.at[:, o + C_NOPE + half:o + C_NOPE + C_ROPE].set(w_uq[:, r:r + half])
        kv0 = h * (C_NOPE + C_DV)
        wk = wk.at[:, o:o + C_NOPE].set(w_ukv[:, kv0:kv0 + C_NOPE])
        wv = wv.at[:, o:o + C_DV].set(w_ukv[:, kv0 + C_NOPE:kv0 + C_NOPE + C_DV])
        for j in range(C_ROPE):
            e1[MISC_CKR + j, o + C_NOPE + j] = 1.0
        for j in range(half):
            e2[MISC_CKR + half + j, o + C_NOPE + j] = -1.0
            e2[MISC_CKR + j, o + C_NOPE + half + j] = 1.0
    return (wq.astype(BF16), wqs.astype(BF16), wk.astype(BF16), wv.astype(BF16),
            jnp.asarray(e1, BF16), jnp.asarray(e2, BF16))


def kernel(x, w_in, w_out, norm_mix, norm_ffn, rel_bias, lam_q1, lam_k1, lam_q2, lam_k2,
           diff_subln, mla_q_norm, mla_kv_norm, w_uq, w_ukv, w_group, b_group, w_expert,
           b_expert, w_gate, w_up, w_down, norm_final):
    bsz, seq, d = x.shape
    n = bsz * seq
    assert d == D_MODEL and seq % TM == 0 and n % TM == 0
    topk = min(DSA_TOPK_MAX, seq // 4)
    assert topk % Q_BLOCK == 0

    cos_t, sin_t = _rope_lane_tables(seq)
    bias_a, far_a = _bias_tiles(rel_bias[:, :A_HEADS])
    bias_b, far_b = _bias_tiles(rel_bias[:, A_HEADS:])
    bias_c, far_c = _bias_tiles(jnp.zeros((REL_BUCKETS, 1), F32))

    n_assign = 2 * n
    n_rows = (n_assign + MOE_ROWS - 1) // MOE_ROWS * MOE_ROWS + N_EXPERTS * MOE_ROWS
    n_blocks = n_rows // MOE_ROWS

    xf = x.reshape(n, d)
    for l in range(DEPTH):
        z2d = _in_proj(xf, norm_mix[l], _prep_w_in(w_in[l]))
        z3d = z2d.reshape(bsz, seq, Z_WIDTH)

        lam_init = 0.8 - 0.6 * math.exp(-0.3 * l)
        lam = (jnp.exp(jnp.sum(lam_q1[l] * lam_k1[l])) - jnp.exp(jnp.sum(lam_q2[l] * lam_k2[l]))
               + lam_init)
        sc_a = jnp.concatenate([jnp.stack([lam, jnp.float32(1.0 - lam_init)]), far_a]).astype(F32)
        wa = A_HEADS * LANES
        ya = _attn_call(
            _diff_attn_kernel, "diff_attn", sc_a, bsz, seq, z3d, ZB_AQ * LANES // wa, wa,
            [(z3d, (1, seq, wa), lambda b, i: (b, 0, ZB_AK * LANES // wa)),
             (z3d, (1, seq, wa), lambda b, i: (b, 0, ZB_AV * LANES // wa))],
            [(diff_subln[l].reshape(1, A_DV), (1, A_DV), lambda b, i: (0, 0))], bias_a, wa)

        wb = B_HEADS * B_DH
        yb = _attn_call(
            functools.partial(_dsa_attn_kernel, topk=topk), "dsa_attn", far_b, bsz, seq,
            z3d, ZB_BQ * LANES // wb, wb,
            [(z3d, (1, seq, wb), lambda b, i: (b, 0, ZB_BK * LANES // wb)),
             (z3d, (1, seq, wb), lambda b, i: (b, 0, ZB_BV * LANES // wb))],
            [(z3d, (1, Q_BLOCK, 2 * LANES), lambda b, i: (b, i, ZB_IQ // 2)),
             (z3d, (1, Q_BLOCK, LANES), lambda b, i: (b, i, ZB_MISC)),
             (z3d, (1, seq, LANES), lambda b, i: (b, 0, ZB_IK))],
            bias_b, wb, scratch=[pltpu.VMEM((Q_BLOCK, seq), F32)])

        wq, wqs, wk, wv, e1, e2 = _prep_mla_weights(w_uq[l], w_ukv[l])
        qc, kc, vc = _mla_prep(z2d, seq, cos_t, sin_t, mla_q_norm[l].reshape(1, -1),
                               mla_kv_norm[l].reshape(1, -1), wq, wqs, wk, wv, e1, e2)
        wc = 2 * LANES
        qc3, kc3, vc3 = (a.reshape(bsz, seq, wc) for a in (qc, kc, vc))
        yc = _attn_call(
            _mla_attn_kernel, "mla_attn", far_c, bsz, seq, qc3, 0, wc,
            [(kc3, (1, seq, wc), lambda b, i: (b, 0, 0)),
             (vc3, (1, seq, wc), lambda b, i: (b, 0, 0))],
            [], bias_c, wc)

        w_router = jnp.concatenate(
            [w_group[l], w_expert[l],
             jnp.zeros((d, LANES - N_GROUPS - N_EXPERTS), F32)], axis=1)
        wr_hi = w_router.astype(BF16)
        wr_lo = (w_router - wr_hi.astype(F32)).astype(BF16)
        b_router = jnp.concatenate(
            [b_group[l], b_expert[l], jnp.zeros((LANES - N_GROUPS - N_EXPERTS,), F32)]).reshape(1, LANES)
        x1, hn, route = _out_router(ya.reshape(n, -1), yb.reshape(n, -1), yc.reshape(n, -1), xf,
                                    w_out[l].astype(BF16), norm_ffn[l].reshape(1, d),
                                    wr_hi, wr_lo, b_router)

        dest, counts = _moe_positions(route)
        dest_flat = dest[:, :2].reshape(-1)
        blocks_per_e = ((counts[0, :N_EXPERTS].astype(jnp.int32) + MOE_ROWS - 1) // MOE_ROWS)
        blk_end = jnp.cumsum(blocks_per_e)
        blk_e = jnp.minimum(jnp.searchsorted(blk_end, jnp.arange(n_blocks), side='right'),
                            N_EXPERTS - 1).astype(jnp.int32)
        n_used = blk_end[-1:].astype(jnp.int32)
        buf = _dispatch(dest_flat, hn, n_rows)
        y = _experts(blk_e, n_used, buf, w_gate[l].astype(BF16), w_up[l].astype(BF16),
                     w_down[l].astype(BF16))
        xf = _combine(dest_flat, y, x1, route, norm_final.reshape(1, d), final=(l == DEPTH - 1))
    return xf.reshape(bsz, seq, d)
```
